```python
import jax, jax.numpy as jnp
from jax import lax
import numpy as np

D_MODEL = 2048
BATCH = 2
SEQ = 16384
DEPTH = 2

HG_HEADS = 8
HG_KDIM = 128
HG_VDIM = 128
HG_QF = HG_HEADS * HG_KDIM
HG_WIDTH = HG_HEADS * HG_VDIM
HG_CHUNK = 64
AT_HEADS = 16
AT_KV_HEADS = 2
AT_HEAD_DIM = 64
AT_GROUP = AT_HEADS // AT_KV_HEADS
AT_WIDTH = AT_HEADS * AT_HEAD_DIM
AT_KV_WIDTH = AT_KV_HEADS * AT_HEAD_DIM
WINDOW = 128
BLOCK = 128
D_FF = 4 * D_MODEL
EPS = 1e-6

SPLIT_SIZES = (HG_QF, HG_QF, HG_WIDTH, HG_WIDTH, AT_WIDTH, AT_KV_WIDTH, AT_KV_WIDTH, D_MODEL, D_MODEL)
D_IN = sum(SPLIT_SIZES)

kernel_name = "hybrid_hgrn2_swa_sink_gated"


def rmsnorm(x, g):
    xf = x.astype(jnp.float32)
    y = xf * lax.rsqrt(jnp.mean(xf * xf, axis=-1, keepdims=True) + EPS)
    return (y * g.astype(jnp.float32)).astype(x.dtype)


def alibi_slopes(n):
    return jnp.asarray(2.0 ** (-8.0 * (np.arange(n, dtype=np.float32) + 1.0) / n), dtype=jnp.float32)


def hgrn2(q, f_logit, i, lb):
    bsz, t_len, _ = q.shape
    n_chunks = t_len // HG_CHUNK
    c = HG_CHUNK
    lb = lb.astype(jnp.float32)
    qf = jax.nn.silu(q.astype(jnp.float32))
    fg = lb + (1.0 - lb) * jax.nn.sigmoid(f_logit.astype(jnp.float32))
    kf = 1.0 - fg
    logf = jnp.log(fg)

    def chunks(a, d):
        return a.reshape(bsz, n_chunks, c, HG_HEADS, d).transpose(0, 1, 3, 2, 4)

    q_c = chunks(qf, HG_KDIM)
    k_c = chunks(kf, HG_KDIM)
    g_c = chunks(logf, HG_KDIM)
    v_c = chunks(i.astype(jnp.float32), HG_VDIM)
    cum = jnp.cumsum(g_c, axis=3)
    ref = cum[:, :, :, c // 2 - 1:c // 2]
    a = jnp.einsum('bnhck,bnhsk->bnhcs', q_c * jnp.exp(cum - ref), k_c * jnp.exp(ref - cum))
    causal = jnp.tril(jnp.ones((c, c), dtype=bool))
    a = jnp.where(causal, a, 0.0)
    o_intra = jnp.einsum('bnhcs,bnhsv->bnhcv', a, v_c)
    last = cum[:, :, :, -1:]
    q_out = q_c * jnp.exp(cum)
    k_state = k_c * jnp.exp(last - cum)
    decay_last = jnp.exp(last[:, :, :, 0, :])

    def step(state, xs):
        qo, ks, vs, dl = xs
        o = jnp.einsum('bhck,bhkv->bhcv', qo, state)
        state = dl[..., None] * state + jnp.einsum('bhck,bhcv->bhkv', ks, vs)
        return state, o

    xs = (jnp.moveaxis(q_out, 1, 0), jnp.moveaxis(k_state, 1, 0),
          jnp.moveaxis(v_c, 1, 0), jnp.moveaxis(decay_last, 1, 0))
    s0 = jnp.zeros((bsz, HG_HEADS, HG_KDIM, HG_VDIM), jnp.float32)
    _, o_inter = lax.scan(step, s0, xs)
    o = o_intra + jnp.moveaxis(o_inter, 0, 1)
    return o.transpose(0, 1, 3, 2, 4).reshape(bsz, t_len, HG_HEADS, HG_VDIM)


def swa_sinks(q, k, v, sinks):
    bsz, t_len = q.shape[0], q.shape[1]
    nb = t_len // BLOCK
    qb = q.reshape(bsz, nb, BLOCK, AT_KV_HEADS, AT_GROUP, AT_HEAD_DIM)
    pad = jnp.zeros((bsz, BLOCK, AT_KV_HEADS, AT_HEAD_DIM), k.dtype)

    def band(a):
        ap = jnp.concatenate([pad, a], axis=1).reshape(bsz, nb + 1, BLOCK, AT_KV_HEADS, AT_HEAD_DIM)
        return jnp.concatenate([ap[:, :-1], ap[:, 1:]], axis=2)

    kb = band(k)
    vb = band(v)
    scale = AT_HEAD_DIM ** -0.5
    s = jnp.einsum('bnqhgd,bnkhd->bnhgqk', qb, kb).astype(jnp.float32) * scale
    qi = jnp.arange(BLOCK)[:, None]
    kj = jnp.arange(2 * BLOCK)[None, :]
    dist = qi + BLOCK - kj
    in_band = (dist >= 0) & (dist < WINDOW)
    blk = jnp.arange(nb)[:, None, None]
    valid = in_band[None] & ((blk > 0) | (kj[None] >= BLOCK))
    slopes = alibi_slopes(AT_HEADS)
    bias = -slopes[:, None, None] * dist.astype(jnp.float32)[None]
    s = s + bias.reshape(AT_KV_HEADS, AT_GROUP, BLOCK, 2 * BLOCK)[None, None]
    s = jnp.where(valid[None, :, None, None], s, -jnp.inf)
    sink = sinks.astype(jnp.float32).reshape(AT_KV_HEADS, AT_GROUP)[None, None, :, :, None, None]
    m = jnp.maximum(jnp.max(s, axis=-1, keepdims=True), sink)
    p = jnp.exp(s - m)
    p = p / (jnp.sum(p, axis=-1, keepdims=True) + jnp.exp(sink - m))
    o = jnp.einsum('bnhgqk,bnkhd->bnqhgd', p.astype(v.dtype), vb)
    return o.reshape(bsz, t_len, AT_WIDTH)


def setup_inputs(seed: int = 0) -> dict:
    key = jax.random.key(seed)
    ks = jax.random.split(key, 14)
    f32 = jnp.float32
    L = DEPTH
    return {
        "x": jax.random.normal(ks[0], (BATCH, SEQ, D_MODEL), f32),
        "norm_mix": 1.0 + 0.1 * jax.random.normal(ks[1], (L, D_MODEL), f32),
        "w_in": jax.random.normal(ks[2], (L, D_MODEL, D_IN), f32) * D_MODEL ** -0.5,
        "lb_logits": 0.5 * jax.random.normal(ks[3], (L, HG_QF), f32),
        "hg_norm": 1.0 + 0.1 * jax.random.normal(ks[4], (L, HG_VDIM), f32),
        "q_norm": 1.0 + 0.1 * jax.random.normal(ks[5], (L, AT_HEAD_DIM), f32),
        "k_norm": 1.0 + 0.1 * jax.random.normal(ks[6], (L, AT_HEAD_DIM), f32),
        "sinks": 0.5 * jax.random.normal(ks[7], (L, AT_HEADS), f32),
        "w_hg_out": jax.random.normal(ks[8], (L, HG_WIDTH, D_MODEL), f32) * HG_WIDTH ** -0.5,
        "w_at_out": jax.random.normal(ks[9], (L, AT_WIDTH, D_MODEL), f32) * AT_WIDTH ** -0.5,
        "w_out": jax.random.normal(ks[10], (L, D_MODEL, D_MODEL), f32) * D_MODEL ** -0.5,
        "norm_ffn": 1.0 + 0.1 * jax.random.normal(ks[11], (L, D_MODEL), f32),
        "w_up": jax.random.normal(ks[12], (L, D_MODEL, D_FF), f32) * D_MODEL ** -0.5,
        "w_down": jax.random.normal(ks[13], (L, D_FF, D_MODEL), f32) * (0.5 * D_FF ** -0.5),
    }


def reference(x, norm_mix, w_in, lb_logits, hg_norm, q_norm, k_norm, sinks,
              w_hg_out, w_at_out, w_out, norm_ffn, w_up, w_down):
    bsz, t_len, _ = x.shape
    lbp = jax.nn.softmax(lb_logits.astype(jnp.float32), axis=0)
    lower_bounds = jnp.cumsum(lbp, axis=0) - lbp[0:1]
    offsets = [int(v) for v in np.cumsum(SPLIT_SIZES)[:-1]]
    for l in range(DEPTH):
        h = rmsnorm(x, norm_mix[l])
        z = h @ w_in[l]
        hq, hf, hi, hgate, aq, ak, av, g_hg, g_at = jnp.split(z, offsets, axis=-1)
        o_hg = hgrn2(hq, hf, hi, lower_bounds[l]).astype(x.dtype)
        o_hg = rmsnorm(o_hg, hg_norm[l]) * jax.nn.silu(hgate.reshape(bsz, t_len, HG_HEADS, HG_VDIM))
        y_hg = o_hg.reshape(bsz, t_len, HG_WIDTH) @ w_hg_out[l]
        qa = rmsnorm(aq.reshape(bsz, t_len, AT_HEADS, AT_HEAD_DIM), q_norm[l])
        ka = rmsnorm(ak.reshape(bsz, t_len, AT_KV_HEADS, AT_HEAD_DIM), k_norm[l])
        va = av.reshape(bsz, t_len, AT_KV_HEADS, AT_HEAD_DIM)
        y_at = swa_sinks(qa, ka, va, sinks[l]) @ w_at_out[l]
        mixed = jax.nn.sigmoid(g_hg) * y_hg + jax.nn.sigmoid(g_at) * y_at
        x = x + mixed @ w_out[l]
        h2 = rmsnorm(x, norm_ffn[l])
        x = x + jnp.square(jax.nn.relu(h2 @ w_up[l])) @ w_down[l]
    return x
```

```python
import functools

import jax
import jax.numpy as jnp
import numpy as np
from jax import lax
from jax.experimental import pallas as pl
from jax.experimental.pallas import tpu as pltpu

F32 = jnp.float32
BF16 = jnp.bfloat16

D_MODEL = 2048
DEPTH = 2
HG_HEADS = 8
HG_DIM = 128
HG_WIDTH = HG_HEADS * HG_DIM
HG_CHUNK = 64
AT_HEADS = 16
AT_KV_HEADS = 2
AT_GROUP = AT_HEADS // AT_KV_HEADS
AT_HEAD_DIM = 64
AT_WIDTH = AT_HEADS * AT_HEAD_DIM
AT_KV_WIDTH = AT_KV_HEADS * AT_HEAD_DIM
WINDOW = 128
D_FF = 4 * D_MODEL
EPS = 1e-6

_OFF_HG = 0
_OFF_KV = 4 * HG_WIDTH + AT_WIDTH
_OFF_GATES = _OFF_KV + 2 * AT_KV_WIDTH
Z_WIDTH = 2 * D_MODEL + 4 * HG_WIDTH + AT_WIDTH
_ZBLK_Q, _ZBLK_F, _ZBLK_I, _ZBLK_GATE, _ZBLK_AQ = 4, 5, 6, 7, 8

LANES = 128
VMEM_LIMIT_BYTES = 56 * 1024 * 1024

INPROJ_TM, INPROJ_TN = 1024, 1024
HGRN2_TB = 512
SWA_TQ = 256
MERGE_TM, MERGE_TN = 256, 512
FFN_TM, FFN_TF = 1024, 512
NORM_ROWS = 128


def _dot(a, b):
    return jnp.dot(a, b, preferred_element_type=F32)


def _dot_nt(a, b):
    return lax.dot_general(a, b, (((1,), (1,)), ((), ())), preferred_element_type=F32)


def _dot_tn(a, b):
    return lax.dot_general(a, b, (((0,), (0,)), ((), ())), preferred_element_type=F32)


def _sigmoid(x):
    return 1.0 / (1.0 + jnp.exp(-x))


def _split_bf16(x):
    hi = x.astype(BF16)
    lo = (x - hi.astype(F32)).astype(BF16)
    return hi, lo


def _rmsnorm_to_bf16(x_ref, g_ref, h_ref):
    g = g_ref[...]

    def body(r, carry):
        rows = pl.ds(pl.multiple_of(r * NORM_ROWS, NORM_ROWS), NORM_ROWS)
        x = x_ref[rows, :]
        ms = jnp.mean(x * x, axis=-1, keepdims=True)
        h_ref[rows, :] = (x * lax.rsqrt(ms + EPS) * g).astype(BF16)
        return carry

    lax.fori_loop(0, x_ref.shape[0] // NORM_ROWS, body, 0)


def _inproj_kernel(x_ref, g_ref, wm_ref, wkv_ref, z_ref, kv_ref, h_ref):
    @pl.when(pl.program_id(1) == 0)
    def _():
        _rmsnorm_to_bf16(x_ref, g_ref, h_ref)
        kv_ref[...] = _dot(h_ref[...], wkv_ref[...])

    z_ref[...] = _dot(h_ref[...], wm_ref[...])


def _inproj(x, g, wm, wkv):
    n = x.shape[0]
    tm, tn = INPROJ_TM, INPROJ_TN
    return pl.pallas_call(
        _inproj_kernel,
        out_shape=(jax.ShapeDtypeStruct((n, Z_WIDTH), F32),
                   jax.ShapeDtypeStruct((n, 2 * AT_KV_WIDTH), F32)),
        grid=(n // tm, Z_WIDTH // tn),
        in_specs=[
            pl.BlockSpec((tm, D_MODEL), lambda i, j: (i, 0)),
            pl.BlockSpec((1, D_MODEL), lambda i, j: (0, 0)),
            pl.BlockSpec((D_MODEL, tn), lambda i, j: (0, j)),
            pl.BlockSpec((D_MODEL, 2 * AT_KV_WIDTH), lambda i, j: (0, 0)),
        ],
        out_specs=(
            pl.BlockSpec((tm, tn), lambda i, j: (i, j)),
            pl.BlockSpec((tm, 2 * AT_KV_WIDTH), lambda i, j: (i, 0)),
        ),
        scratch_shapes=[pltpu.VMEM((tm, D_MODEL), BF16)],
        compiler_params=pltpu.CompilerParams(
            dimension_semantics=("parallel", "arbitrary"),
            vmem_limit_bytes=VMEM_LIMIT_BYTES),
        name="inproj",
    )(x, g, wm, wkv)


def _hgrn2_kernel(lbl_ref, gn_ref, q_ref, f_ref, i_ref, gate_ref, o_ref, st_ref, *, layer):
    @pl.when(pl.program_id(1) == 0)
    def _():
        st_ref[...] = jnp.zeros_like(st_ref)

    lbl = lbl_ref[...]
    e = jnp.exp(lbl - jnp.max(lbl, axis=0, keepdims=True))
    p = e / jnp.sum(e, axis=0, keepdims=True)
    lb = jnp.zeros((1, HG_WIDTH), F32)
    for l in range(1, layer + 1):
        lb = lb + p[l:l + 1, :]

    c = HG_CHUNK
    row = lax.broadcasted_iota(jnp.int32, (c, c), 0)
    col = lax.broadcasted_iota(jnp.int32, (c, c), 1)
    causal = row >= col
    tril = causal.astype(BF16)
    gn = gn_ref[...]

    def chunk(ci, carry):
        rows = pl.ds(pl.multiple_of(ci * c, c), c)
        q = q_ref[rows, :]
        qf = q * _sigmoid(q)
        fg = lb + (1.0 - lb) * _sigmoid(f_ref[rows, :])
        kf = 1.0 - fg
        hi, lo = _split_bf16(jnp.log(fg))
        cum = _dot(tril, hi) + _dot(tril, lo)
        ref = cum[c // 2 - 1:c // 2, :]
        last = cum[c - 1:c, :]
        qa = (qf * jnp.exp(cum - ref)).astype(BF16)
        ka = (kf * jnp.exp(ref - cum)).astype(BF16)
        qo = (qf * jnp.exp(cum)).astype(BF16)
        ks = (kf * jnp.exp(last - cum)).astype(BF16)
        dl = jnp.exp(last)
        vb = i_ref[rows, :].astype(BF16)
        for h in range(HG_HEADS):
            sl = slice(h * HG_DIM, (h + 1) * HG_DIM)
            a = jnp.where(causal, _dot_nt(qa[:, sl], ka[:, sl]), 0.0).astype(BF16)
            st = st_ref[h]
            o = _dot(a, vb[:, sl]) + _dot_nt(qo[:, sl], st.astype(BF16))
            st_ref[h] = st * dl[:, sl] + _dot_tn(vb[:, sl], ks[:, sl])
            ms = jnp.mean(o * o, axis=-1, keepdims=True)
            g = gate_ref[rows, sl]
            y = o * lax.rsqrt(ms + EPS) * gn[:, sl]
            o_ref[rows, sl] = (y * (g * _sigmoid(g))).astype(o_ref.dtype)
        return carry

    lax.fori_loop(0, q_ref.shape[0] // c, chunk, 0)


def _hgrn2(z, lb_logits, gn, layer, bsz, t_len):
    n = z.shape[0]
    tb = HGRN2_TB
    nt = t_len // tb

    def zspec(blk):
        return pl.BlockSpec((tb, HG_WIDTH), lambda b, t: (b * nt + t, blk))

    return pl.pallas_call(
        functools.partial(_hgrn2_kernel, layer=layer),
        out_shape=jax.ShapeDtypeStruct((n, HG_WIDTH), BF16),
        grid=(bsz, nt),
        in_specs=[
            pl.BlockSpec((DEPTH, HG_WIDTH), lambda b, t: (0, 0)),
            pl.BlockSpec((1, HG_WIDTH), lambda b, t: (0, 0)),
            zspec(_ZBLK_Q), zspec(_ZBLK_F), zspec(_ZBLK_I), zspec(_ZBLK_GATE),
        ],
        out_specs=pl.BlockSpec((tb, HG_WIDTH), lambda b, t: (b * nt + t, 0)),
        scratch_shapes=[pltpu.VMEM((HG_HEADS, HG_DIM, HG_DIM), F32)],
        compiler_params=pltpu.CompilerParams(
            dimension_semantics=("parallel", "arbitrary"),
            vmem_limit_bytes=VMEM_LIMIT_BYTES),
        name="hgrn2",
    )(lb_logits, gn, z, z, z, z)


def _alibi_slope(h):
    return float(np.float32(2.0) ** np.float32(-8.0 * (h + 1.0) / AT_HEADS))


def _swa_kernel(sink_ref, qn_ref, kn_ref, q_ref, kvc_ref, kvp_ref, o_ref):
    w = WINDOW
    tq = q_ref.shape[0]
    lane = lax.broadcasted_iota(jnp.int32, (1, LANES), 1)
    lo_half = lane < AT_HEAD_DIM
    r = lax.broadcasted_iota(jnp.int32, (LANES, LANES), 0)
    cidx = lax.broadcasted_iota(jnp.int32, (LANES, LANES), 1)
    seg_ones = ((r < AT_HEAD_DIM) == (cidx < AT_HEAD_DIM)).astype(BF16)

    def segnorm(x, g):
        hi, lo = _split_bf16(x * x)
        ssq = _dot(hi, seg_ones) + _dot(lo, seg_ones)
        return x * lax.rsqrt(ssq * (1.0 / AT_HEAD_DIM) + EPS) * g

    kvc = kvc_ref[...]
    kvp = kvp_ref[...]
    k_all = segnorm(jnp.concatenate([kvp[:, :LANES], kvc[:, :LANES]], axis=0), kn_ref[...])
    v_all = jnp.concatenate([kvp[:, LANES:], kvc[:, LANES:]], axis=0)
    k_sw = pltpu.roll(k_all, AT_HEAD_DIM, 1)
    v_sw = pltpu.roll(v_all, AT_HEAD_DIM, 1)
    k2 = (jnp.where(lo_half, k_all, k_sw).astype(BF16), jnp.where(lo_half, k_sw, k_all).astype(BF16))
    v2 = (jnp.where(lo_half, v_all, v_sw).astype(BF16), jnp.where(lo_half, v_sw, v_all).astype(BF16))

    qi = lax.broadcasted_iota(jnp.int32, (w, 2 * w), 0)
    kj = lax.broadcasted_iota(jnp.int32, (w, 2 * w), 1)
    dist = qi + w - kj
    in_band = (dist >= 0) & (dist < w)
    distf = dist.astype(F32)
    kj_min = jnp.where(pl.program_id(1) == 0, w, 0)
    scale = AT_HEAD_DIM ** -0.5
    qg = qn_ref[...]

    for qb in range(tq // w):
        rows = slice(qb * w, (qb + 1) * w)
        valid = (in_band & (kj >= kj_min)) if qb == 0 else in_band
        for kvh in range(AT_KV_HEADS):
            keys = k2[kvh][qb * w:(qb + 2) * w]
            vals = v2[kvh][qb * w:(qb + 2) * w]
            parts = []
            for j in range(AT_GROUP // 2):
                tile = kvh * (AT_GROUP // 2) + j
                qt = segnorm(q_ref[rows, tile * LANES:(tile + 1) * LANES], qg)
                parts.append(jnp.where(lo_half, qt, 0.0))
                parts.append(jnp.where(lo_half, 0.0, qt))
            s_all = _dot_nt(jnp.concatenate(parts, axis=0).astype(BF16), keys)
            outs = []
            for g in range(AT_GROUP):
                head = kvh * AT_GROUP + g
                s = s_all[g * w:(g + 1) * w] * scale - _alibi_slope(head) * distf
                s = jnp.where(valid, s, -jnp.inf)
                sink = sink_ref[head]
                m = jnp.maximum(jnp.max(s, axis=-1, keepdims=True), sink)
                p = jnp.exp(s - m)
                den = jnp.sum(p, axis=-1, keepdims=True) + jnp.exp(sink - m)
                outs.append(_dot(p.astype(BF16), vals) / den)
            for j in range(AT_GROUP // 2):
                tile = kvh * (AT_GROUP // 2) + j
                o_ref[rows, tile * LANES:(tile + 1) * LANES] = jnp.where(
                    lo_half, outs[2 * j], outs[2 * j + 1]).astype(o_ref.dtype)


def _swa(z, kv, qn, kn, sinks, bsz, t_len):
    n = z.shape[0]
    tq = SWA_TQ
    nt = t_len // tq
    per = tq // WINDOW
    return pl.pallas_call(
        _swa_kernel,
        out_shape=jax.ShapeDtypeStruct((n, AT_WIDTH), BF16),
        grid=(bsz, nt),
        in_specs=[
            pl.BlockSpec(memory_space=pltpu.SMEM),
            pl.BlockSpec((1, LANES), lambda b, t: (0, 0)),
            pl.BlockSpec((1, LANES), lambda b, t: (0, 0)),
            pl.BlockSpec((tq, AT_WIDTH), lambda b, t: (b * nt + t, _ZBLK_AQ)),
            pl.BlockSpec((tq, 2 * AT_KV_WIDTH), lambda b, t: (b * nt + t, 0)),
            pl.BlockSpec((WINDOW, 2 * AT_KV_WIDTH),
                         lambda b, t: (jnp.maximum((b * nt + t) * per - 1, 0), 0)),
        ],
        out_specs=pl.BlockSpec((tq, AT_WIDTH), lambda b, t: (b * nt + t, 0)),
        compiler_params=pltpu.CompilerParams(
            dimension_semantics=("parallel", "arbitrary"),
            vmem_limit_bytes=VMEM_LIMIT_BYTES),
        name="swa",
    )(sinks, qn, kn, z, kv, kv)


def _merge_kernel(x_ref, oh_ref, oa_ref, gh_ref, ga_ref, wh_ref, wa_ref, wo_ref, out_ref, mix_ref):
    oh = oh_ref[...]
    oa = oa_ref[...]
    for ci in range(D_MODEL // MERGE_TN):
        cs = slice(ci * MERGE_TN, (ci + 1) * MERGE_TN)
        yh = _dot(oh, wh_ref[:, cs])
        ya = _dot(oa, wa_ref[:, cs])
        mix_ref[:, cs] = (_sigmoid(gh_ref[:, cs]) * yh + _sigmoid(ga_ref[:, cs]) * ya).astype(BF16)
    out_ref[...] = x_ref[...] + _dot(mix_ref[...], wo_ref[...])


def _merge(x, oh, oa, z, wh, wa, wo):
    n = x.shape[0]
    tm = MERGE_TM
    resident = dict(pipeline_mode=pl.Buffered(1))
    return pl.pallas_call(
        _merge_kernel,
        out_shape=jax.ShapeDtypeStruct((n, D_MODEL), F32),
        grid=(n // tm,),
        in_specs=[
            pl.BlockSpec((tm, D_MODEL), lambda i: (i, 0)),
            pl.BlockSpec((tm, HG_WIDTH), lambda i: (i, 0)),
            pl.BlockSpec((tm, AT_WIDTH), lambda i: (i, 0)),
            pl.BlockSpec((tm, D_MODEL), lambda i: (i, 0)),
            pl.BlockSpec((tm, D_MODEL), lambda i: (i, 1)),
            pl.BlockSpec((HG_WIDTH, D_MODEL), lambda i: (0, 0), **resident),
            pl.BlockSpec((AT_WIDTH, D_MODEL), lambda i: (0, 0), **resident),
            pl.BlockSpec((D_MODEL, D_MODEL), lambda i: (0, 0), **resident),
        ],
        out_specs=pl.BlockSpec((tm, D_MODEL), lambda i: (i, 0)),
        scratch_shapes=[pltpu.VMEM((tm, D_MODEL), BF16)],
        compiler_params=pltpu.CompilerParams(
            dimension_semantics=("parallel",),
            vmem_limit_bytes=VMEM_LIMIT_BYTES),
        name="merge",
    )(x, oh, oa, z, z, wh, wa, wo)


def _ffn_kernel(x_ref, g_ref, wu_ref, wd_ref, out_ref, h_ref):
    @pl.when(pl.program_id(1) == 0)
    def _():
        _rmsnorm_to_bf16(x_ref, g_ref, h_ref)
        out_ref[...] = x_ref[...]

    u = jnp.maximum(_dot(h_ref[...], wu_ref[...]), 0.0)
    out_ref[...] += _dot((u * u).astype(BF16), wd_ref[...])


def _ffn(x, g, wu, wd):
    n = x.shape[0]
    tm, tf = FFN_TM, FFN_TF
    return pl.pallas_call(
        _ffn_kernel,
        out_shape=jax.ShapeDtypeStruct((n, D_MODEL), F32),
        grid=(n // tm, D_FF // tf),
        in_specs=[
            pl.BlockSpec((tm, D_MODEL), lambda i, f: (i, 0)),
            pl.BlockSpec((1, D_MODEL), lambda i, f: (0, 0)),
            pl.BlockSpec((D_MODEL, tf), lambda i, f: (0, f)),
            pl.BlockSpec((tf, D_MODEL), lambda i, f: (f, 0)),
        ],
        out_specs=pl.BlockSpec((tm, D_MODEL), lambda i, f: (i, 0)),
        scratch_shapes=[pltpu.VMEM((tm, D_MODEL), BF16)],
        compiler_params=pltpu.CompilerParams(
            dimension_semantics=("parallel", "arbitrary"),
            vmem_limit_bytes=VMEM_LIMIT_BYTES),
        name="ffn",
    )(x, g, wu, wd)


def kernel(x, norm_mix, w_in, lb_logits, hg_norm, q_norm, k_norm, sinks,
           w_hg_out, w_at_out, w_out, norm_ffn, w_up, w_down):
    bsz, t_len, _ = x.shape
    assert t_len % HGRN2_TB == 0 and t_len % SWA_TQ == 0
    assert (bsz * t_len) % INPROJ_TM == 0 and (bsz * t_len) % FFN_TM == 0
    xf = x.reshape(bsz * t_len, D_MODEL)
    for l in range(DEPTH):
        w = w_in[l]
        wm = jnp.concatenate([w[:, _OFF_GATES:], w[:, _OFF_HG:_OFF_KV]], axis=1).astype(BF16)
        wkv = w[:, _OFF_KV:_OFF_GATES].astype(BF16)
        z, kv = _inproj(xf, norm_mix[l].reshape(1, D_MODEL), wm, wkv)
        o_hg = _hgrn2(z, lb_logits, jnp.tile(hg_norm[l], HG_HEADS).reshape(1, HG_WIDTH), l,
                      bsz, t_len)
        o_at = _swa(z, kv, jnp.tile(q_norm[l], 2).reshape(1, LANES),
                    jnp.tile(k_norm[l], 2).reshape(1, LANES), sinks[l], bsz, t_len)
        x1 = _merge(xf, o_hg, o_at, z, w_hg_out[l].astype(BF16), w_at_out[l].astype(BF16),
                    w_out[l].astype(BF16))
        xf = _ffn(x1, norm_ffn[l].reshape(1, D_MODEL), w_up[l].astype(BF16),
                  w_down[l].astype(BF16))
    return xf.reshape(bsz, t_len, D_MODEL)
```

```python
import functools
from typing import Callable, NamedTuple

import jax
import jax.numpy as jnp
import numpy as np
from jax import lax
from jax.experimental import pallas as pl
from jax.experimental.pallas import tpu as pltpu

F32 = jnp.float32
BF16 = jnp.bfloat16

D_MODEL = 2048
DEPTH = 2
HG_HEADS = 8
HG_DIM = 128
HG_WIDTH = HG_HEADS * HG_DIM
HG_CHUNK = 64
AT_HEADS = 16
AT_KV_HEADS = 2
AT_GROUP = AT_HEADS // AT_KV_HEADS
AT_HEAD_DIM = 64
AT_WIDTH = AT_HEADS * AT_HEAD_DIM
AT_KV_WIDTH = AT_KV_HEADS * AT_HEAD_DIM
WINDOW = 128
D_FF = 4 * D_MODEL
EPS = 1e-6
LOG2E = 1.4426950408889634

_OFF_HG = 0
_OFF_KV = 4 * HG_WIDTH + AT_WIDTH
_OFF_GATES = _OFF_KV + 2 * AT_KV_WIDTH
Z_WIDTH = 2 * D_MODEL + 4 * HG_WIDTH + AT_WIDTH
Z_DTYPE = BF16
_ZBLK_Q, _ZBLK_F, _ZBLK_I, _ZBLK_GATE, _ZBLK_AQ = 4, 5, 6, 7, 8
_HALVED_BLOCKS = (0, 1, 3)

LANES = 128
VMEM_LIMIT_BYTES = 60 * 1024 * 1024

INPROJ_TM, INPROJ_TN = 1024, 2304
HGRN2_TB = 1024
SWA_TQ = 512
MERGE_TM, MERGE_TN = 512, 512
FFN_TM, FFN_TF, FFN_SUB = 1024, 1024, 512
NORM_ROWS = 256


def _dot(a, b):
    return jnp.dot(a, b, preferred_element_type=F32)


def _dot_nt(a, b):
    return lax.dot_general(a, b, (((1,), (1,)), ((), ())), preferred_element_type=F32)


def _dot_tn(a, b):
    return lax.dot_general(a, b, (((0,), (0,)), ((), ())), preferred_element_type=F32)


def _sigmoid(x):
    return 0.5 + 0.5 * jnp.tanh(0.5 * x)


def _silu(x):
    h = 0.5 * x
    return h * (1.0 + jnp.tanh(h))


def _split_bf16(x):
    hi = x.astype(BF16)
    lo = (x - hi.astype(F32)).astype(BF16)
    return hi, lo


def _rmsnorm_rows(x, g):
    ms = jnp.mean(x * x, axis=-1, keepdims=True)
    return (x * lax.rsqrt(ms + EPS) * g).astype(BF16)


def _row_blocks(n):
    return [slice(r, r + NORM_ROWS) for r in range(0, n, NORM_ROWS)]


class _Ride(NamedTuple):
    arrays: tuple
    in_specs: tuple
    out_shapes: tuple
    out_specs: tuple
    body: Callable


def _cast_body(w_ref, o_ref):
    o_ref[...] = w_ref[...].astype(o_ref.dtype)


def _w_in_body(w_ref, wm_ref, wkv_ref):
    wm_ref[:, :2 * D_MODEL] = w_ref[:, _OFF_GATES:].astype(BF16)
    for k in range(5):
        src = slice(k * HG_WIDTH, (k + 1) * HG_WIDTH)
        dst = slice(2 * D_MODEL + k * HG_WIDTH, 2 * D_MODEL + (k + 1) * HG_WIDTH)
        scale = 0.5 if k in _HALVED_BLOCKS else 1.0
        wm_ref[:, dst] = (w_ref[:, src] * scale).astype(BF16)
    wkv_ref[...] = w_ref[:, _OFF_KV:_OFF_GATES].astype(BF16)


def _ride_cast(w, layer, nblk, block_of):
    _, rows, cols = w.shape
    rb = rows // nblk
    assert rb * nblk == rows and rb % 16 == 0
    return _Ride(
        (w,),
        (pl.BlockSpec((None, rb, cols), lambda *g: (layer, block_of(*g), 0)),),
        (jax.ShapeDtypeStruct((rows, cols), BF16),),
        (pl.BlockSpec((rb, cols), lambda *g: (block_of(*g), 0)),),
        _cast_body)


def _ride_w_in(w_in, layer, nblk, block_of):
    _, rows, cols = w_in.shape
    rb = rows // nblk
    assert rb * nblk == rows and rb % 16 == 0
    return _Ride(
        (w_in,),
        (pl.BlockSpec((None, rb, cols), lambda *g: (layer, block_of(*g), 0)),),
        (jax.ShapeDtypeStruct((rows, Z_WIDTH), BF16),
         jax.ShapeDtypeStruct((rows, 2 * AT_KV_WIDTH), BF16)),
        (pl.BlockSpec((rb, Z_WIDTH), lambda *g: (block_of(*g), 0)),
         pl.BlockSpec((rb, 2 * AT_KV_WIDTH), lambda *g: (block_of(*g), 0))),
        _w_in_body)


def _pallas_call_with_rides(body, rides, args, *, in_specs, out_shape, out_specs,
                            once_per_row_tile=False, **kwargs):
    n_in, n_out = len(in_specs), len(out_shape)
    n_rin = [len(r.arrays) for r in rides]
    n_rout = [len(r.out_shapes) for r in rides]

    def kernel(*refs):
        pos = n_in
        r_ins = []
        for k in n_rin:
            r_ins.append(refs[pos:pos + k])
            pos += k
        outs = refs[pos:pos + n_out]
        pos += n_out
        r_outs = []
        for k in n_rout:
            r_outs.append(refs[pos:pos + k])
            pos += k
        body(*refs[:n_in], *outs, *refs[pos:])

        def run_rides():
            for r, i_refs, o_refs in zip(rides, r_ins, r_outs):
                r.body(*i_refs, *o_refs)

        if rides and once_per_row_tile:
            pl.when(pl.program_id(1) == 0)(run_rides)
        else:
            run_rides()

    return pl.pallas_call(
        kernel,
        in_specs=list(in_specs) + [s for r in rides for s in r.in_specs],
        out_shape=tuple(out_shape) + tuple(s for r in rides for s in r.out_shapes),
        out_specs=tuple(out_specs) + tuple(s for r in rides for s in r.out_specs),
        **kwargs,
    )(*args, *[a for r in rides for a in r.arrays])


def _convert_w_in(w_in, layer):
    nblk = 16
    ride = _ride_w_in(w_in, layer, nblk, lambda i: i)
    return pl.pallas_call(
        _w_in_body,
        grid=(nblk,),
        in_specs=list(ride.in_specs),
        out_shape=ride.out_shapes,
        out_specs=ride.out_specs,
        compiler_params=pltpu.CompilerParams(
            dimension_semantics=("parallel",),
            vmem_limit_bytes=VMEM_LIMIT_BYTES),
        name="convert_w_in",
    )(w_in)


def _inproj_kernel(x_ref, g_ref, wm_ref, wkv_ref, z_ref, kv_ref, h_ref):
    first = pl.program_id(1) == 0

    @pl.when(first)
    def _():
        g = g_ref[...]
        for rows in _row_blocks(x_ref.shape[0]):
            h = _rmsnorm_rows(x_ref[rows, :], g)
            h_ref[rows, :] = h
            kv_ref[rows, :] = _dot(h, wkv_ref[...])
            z_ref[rows, :] = _dot(h, wm_ref[...]).astype(z_ref.dtype)

    @pl.when(jnp.logical_not(first))
    def _():
        z_ref[...] = _dot(h_ref[...], wm_ref[...]).astype(z_ref.dtype)


def _inproj(x, g, wm, wkv, ride_weights, layer):
    n = x.shape[0]
    tm, tn = INPROJ_TM, INPROJ_TN
    rides = [_ride_cast(w, layer, n // tm, lambda i, j: i) for w in ride_weights]
    return _pallas_call_with_rides(
        _inproj_kernel, rides, (x, g, wm, wkv),
        once_per_row_tile=True,
        out_shape=(jax.ShapeDtypeStruct((n, Z_WIDTH), Z_DTYPE),
                   jax.ShapeDtypeStruct((n, 2 * AT_KV_WIDTH), F32)),
        grid=(n // tm, Z_WIDTH // tn),
        in_specs=[
            pl.BlockSpec((tm, D_MODEL), lambda i, j: (i, 0)),
            pl.BlockSpec((1, D_MODEL), lambda i, j: (0, 0)),
            pl.BlockSpec((D_MODEL, tn), lambda i, j: (0, j)),
            pl.BlockSpec((D_MODEL, 2 * AT_KV_WIDTH), lambda i, j: (0, 0)),
        ],
        out_specs=(
            pl.BlockSpec((tm, tn), lambda i, j: (i, j)),
            pl.BlockSpec((tm, 2 * AT_KV_WIDTH), lambda i, j: (i, 0)),
        ),
        scratch_shapes=[pltpu.VMEM((tm, D_MODEL), BF16)],
        compiler_params=pltpu.CompilerParams(
            dimension_semantics=("parallel", "arbitrary"),
            vmem_limit_bytes=VMEM_LIMIT_BYTES),
        name="inproj",
    )


def _hgrn2_kernel(lbl_ref, q_ref, f_ref, i_ref, o_ref, st_ref, ops_ref, dl_ref, *, layer):
    @pl.when(pl.program_id(1) == 0)
    def _():
        st_ref[...] = jnp.zeros_like(st_ref)

    lbl = lbl_ref[...]
    e = jnp.exp(lbl - jnp.max(lbl, axis=0, keepdims=True))
    p = e / jnp.sum(e, axis=0, keepdims=True)
    lb = jnp.zeros((1, HG_WIDTH), F32)
    for l in range(1, layer + 1):
        lb = lb + p[l:l + 1, :]
    fb = 0.5 * (1.0 - lb)
    fa = lb + fb
    fc = 1.0 - fa

    c = HG_CHUNK
    row = lax.broadcasted_iota(jnp.int32, (c, c), 0)
    col = lax.broadcasted_iota(jnp.int32, (c, c), 1)
    causal = row >= col
    tril = causal.astype(BF16)

    def operands(rows, slot):
        hq = q_ref[rows, :].astype(F32)
        qf = hq * (1.0 + jnp.tanh(hq))
        bt = fb * jnp.tanh(f_ref[rows, :].astype(F32))
        fg = fa + bt
        kf = fc - bt
        hi, lo = _split_bf16(jnp.log(fg) * LOG2E)
        cum = _dot(tril, hi) + _dot(tril, lo)
        ref = cum[c // 2 - 1:c // 2, :]
        last = cum[c - 1:c, :]
        e1 = jnp.exp2(cum - ref)
        qa_f = qf * e1
        ka_f = kf / e1
        ops_ref[slot, 0] = qa_f.astype(BF16)
        ops_ref[slot, 1] = ka_f.astype(BF16)
        ops_ref[slot, 2] = (qa_f * jnp.exp2(ref)).astype(BF16)
        ops_ref[slot, 3] = (ka_f * jnp.exp2(last - ref)).astype(BF16)
        dl_ref[slot] = jnp.exp2(last)

    def recurrence(rows, slot):
        qa, ka, qo, ks = (ops_ref[slot, k] for k in range(4))
        dl = dl_ref[slot]
        vb = i_ref[rows, :].astype(BF16)
        heads = [slice(h * HG_DIM, (h + 1) * HG_DIM) for h in range(HG_HEADS)]
        a = [_dot_nt(qa[:, sl], ka[:, sl]) for sl in heads]
        st = [st_ref[h] for h in range(HG_HEADS)]
        o_inter = [_dot_nt(qo[:, sl], st[h].astype(BF16)) for h, sl in enumerate(heads)]
        upd = [_dot_tn(vb[:, sl], ks[:, sl]) for sl in heads]
        for h, sl in enumerate(heads):
            st_ref[h] = st[h] * dl[:, sl] + upd[h]
        a = [jnp.where(causal, x, 0.0).astype(BF16) for x in a]
        for h, sl in enumerate(heads):
            o_ref[rows, sl] = _dot(a[h], vb[:, sl]) + o_inter[h]

    n_chunks = q_ref.shape[0] // c
    operands(slice(0, c), 0)
    for k in range(n_chunks):
        if k + 1 < n_chunks:
            operands(slice((k + 1) * c, (k + 2) * c), (k + 1) % 2)
        recurrence(slice(k * c, (k + 1) * c), k % 2)


def _hgrn2(z, lb_logits, layer, bsz, t_len, ride_weights=()):
    n = z.shape[0]
    tb = HGRN2_TB
    nt = t_len // tb
    rides = [_ride_cast(w, layer, bsz * nt, lambda b, t: b * nt + t) for w in ride_weights]

    def zspec(blk):
        return pl.BlockSpec((tb, HG_WIDTH), lambda b, t: (b * nt + t, blk))

    return _pallas_call_with_rides(
        functools.partial(_hgrn2_kernel, layer=layer), rides, (lb_logits, z, z, z),
        out_shape=(jax.ShapeDtypeStruct((n, HG_WIDTH), F32),),
        grid=(bsz, nt),
        in_specs=[
            pl.BlockSpec((DEPTH, HG_WIDTH), lambda b, t: (0, 0)),
            zspec(_ZBLK_Q), zspec(_ZBLK_F), zspec(_ZBLK_I),
        ],
        out_specs=(pl.BlockSpec((tb, HG_WIDTH), lambda b, t: (b * nt + t, 0)),),
        scratch_shapes=[pltpu.VMEM((HG_HEADS, HG_DIM, HG_DIM), F32),
                        pltpu.VMEM((2, 4, HG_CHUNK, HG_WIDTH), BF16),
                        pltpu.VMEM((2, 1, HG_WIDTH), F32)],
        compiler_params=pltpu.CompilerParams(
            dimension_semantics=("parallel", "arbitrary"),
            vmem_limit_bytes=VMEM_LIMIT_BYTES),
        name="hgrn2",
    )


def _alibi_slope(h):
    return float(np.float32(2.0) ** np.float32(-8.0 * (h + 1.0) / AT_HEADS))


def _swa_kernel(sink_ref, qn_ref, kn_ref, q_ref, kvc_ref, kvp_ref, o_ref, bias_ref):
    w = WINDOW
    tq = q_ref.shape[0]
    first_of_seq = pl.program_id(1) == 0

    @pl.when((pl.program_id(0) == 0) & first_of_seq)
    def _():
        qi = lax.broadcasted_iota(jnp.int32, (w, 2 * w), 0)
        kj = lax.broadcasted_iota(jnp.int32, (w, 2 * w), 1)
        dist = qi + w - kj
        in_band = (dist >= 0) & (dist < w)
        in_band_cur = in_band & (kj >= w)
        distf = dist.astype(F32)
        for h in range(AT_HEADS):
            b = (-_alibi_slope(h) * LOG2E) * distf
            sink = sink_ref[h] * LOG2E
            bias_ref[0, h] = jnp.where(kj == 0, sink, jnp.where(in_band, b, -jnp.inf))
            bias_ref[1, h] = jnp.where(kj == 0, sink, jnp.where(in_band_cur, b, -jnp.inf))

    lane = lax.broadcasted_iota(jnp.int32, (1, LANES), 1)
    lo_half = lane < AT_HEAD_DIM
    r = lax.broadcasted_iota(jnp.int32, (2 * LANES, 2 * LANES), 0)
    cidx = lax.broadcasted_iota(jnp.int32, (2 * LANES, 2 * LANES), 1)
    same_head = lax.shift_right_logical(r, 6) == lax.shift_right_logical(cidx, 6)
    seg_mean = jnp.where(same_head, 1.0 / AT_HEAD_DIM, 0.0).astype(BF16)

    kvc = kvc_ref[...]
    kvp = kvp_ref[...]
    k_all = jnp.concatenate([kvp[:, :LANES], kvc[:, :LANES]], axis=0)
    v_all = jnp.concatenate([kvp[:, LANES:], kvc[:, LANES:]], axis=0)
    hi, lo = _split_bf16(k_all * k_all)
    k_msq = _dot(hi, seg_mean[:LANES, :LANES]) + _dot(lo, seg_mean[:LANES, :LANES])
    kg = kn_ref[...] * qn_ref[...] * (AT_HEAD_DIM ** -0.5 * LOG2E)
    kn = k_all * lax.rsqrt(k_msq + EPS) * kg
    k_sw = pltpu.roll(kn, AT_HEAD_DIM, 1)
    v_sw = pltpu.roll(v_all, AT_HEAD_DIM, 1)
    keys = ((jnp.where(lo_half, kn, 0.0).astype(BF16), jnp.where(lo_half, 0.0, k_sw).astype(BF16)),
            (jnp.where(lo_half, k_sw, 0.0).astype(BF16), jnp.where(lo_half, 0.0, kn).astype(BF16)))
    vals = ((jnp.where(lo_half, v_all, 1.0).astype(BF16), jnp.where(lo_half, 1.0, v_sw).astype(BF16)),
            (jnp.where(lo_half, v_sw, 1.0).astype(BF16), jnp.where(lo_half, 1.0, v_all).astype(BF16)))

    row0 = lax.broadcasted_iota(jnp.int32, (16, LANES), 0) == 0
    lo16 = lax.broadcasted_iota(jnp.int32, (16, LANES), 1) < AT_HEAD_DIM
    zero16 = jnp.zeros((16, LANES), BF16)

    def with_row0_zeroed(x, where_zero):
        return jnp.concatenate([jnp.where(where_zero, zero16, x[:16]), x[16:]], axis=0)

    half = AT_GROUP // 2
    units = [(qb, kvh, par) for qb in range(tq // w) for kvh in range(AT_KV_HEADS) for par in range(2)]
    scores = {}
    for qb in range(tq // w):
        rows = slice(qb * w, (qb + 1) * w)
        krows = slice(qb * w, (qb + 2) * w)
        qh = []
        for pair in range(AT_WIDTH // (2 * LANES)):
            x = q_ref[rows, pair * 2 * LANES:(pair + 1) * 2 * LANES].astype(F32)
            msq = _dot((x * x).astype(BF16), seg_mean)
            xn = (x * lax.rsqrt(msq + EPS)).astype(BF16)
            qh += [xn[:, :LANES], xn[:, LANES:]]
        for kvh in range(AT_KV_HEADS):
            lhs = jnp.concatenate(qh[kvh * half:(kvh + 1) * half], axis=0)
            for par in range(2):
                k_blk = with_row0_zeroed(keys[kvh][par][krows], row0)
                scores[qb, kvh, par] = _dot_nt(lhs, k_blk)
    res = {}
    for qb, kvh, par in units:
        krows = slice(qb * w, (qb + 2) * w)
        sel = jnp.where(first_of_seq, 1, 0) if qb == 0 else 0
        v_blk = with_row0_zeroed(vals[kvh][par][krows], row0 & (lo16 if par == 0 else ~lo16))
        pb = []
        for j in range(half):
            head = kvh * AT_GROUP + 2 * j + par
            s = scores[qb, kvh, par][j * w:(j + 1) * w] + bias_ref[sel, head]
            pb.append(jnp.exp2(s - jnp.max(s, axis=-1, keepdims=True)).astype(BF16))
        res[qb, kvh, par] = _dot(jnp.concatenate(pb, axis=0), v_blk)
    for qb in range(tq // w):
        rows = slice(qb * w, (qb + 1) * w)
        for kvh in range(AT_KV_HEADS):
            for j in range(half):
                tile = kvh * half + j
                re = res[qb, kvh, 0][j * w:(j + 1) * w]
                ro = res[qb, kvh, 1][j * w:(j + 1) * w]
                num = jnp.where(lo_half, re, ro)
                den = pltpu.roll(jnp.where(lo_half, ro, re), AT_HEAD_DIM, 1)
                o_ref[rows, tile * LANES:(tile + 1) * LANES] = (num / den).astype(o_ref.dtype)


def _swa(z, kv, qn, kn, sinks, bsz, t_len, ride_weights=(), layer=0):
    n = z.shape[0]
    tq = SWA_TQ
    nt = t_len // tq
    per = tq // WINDOW
    rides = [_ride_cast(w, layer, bsz * nt, lambda b, t: b * nt + t) for w in ride_weights]
    return _pallas_call_with_rides(
        _swa_kernel, rides, (sinks, qn, kn, z, kv, kv),
        out_shape=(jax.ShapeDtypeStruct((n, AT_WIDTH), BF16),),
        grid=(bsz, nt),
        in_specs=[
            pl.BlockSpec(memory_space=pltpu.SMEM),
            pl.BlockSpec((1, LANES), lambda b, t: (0, 0)),
            pl.BlockSpec((1, LANES), lambda b, t: (0, 0)),
            pl.BlockSpec((tq, AT_WIDTH), lambda b, t: (b * nt + t, _ZBLK_AQ)),
            pl.BlockSpec((tq, 2 * AT_KV_WIDTH), lambda b, t: (b * nt + t, 0)),
            pl.BlockSpec((WINDOW, 2 * AT_KV_WIDTH),
                         lambda b, t: (jnp.maximum((b * nt + t) * per - 1, 0), 0)),
        ],
        out_specs=(pl.BlockSpec((tq, AT_WIDTH), lambda b, t: (b * nt + t, 0)),),
        scratch_shapes=[pltpu.VMEM((2, AT_HEADS, WINDOW, 2 * WINDOW), F32)],
        compiler_params=pltpu.CompilerParams(
            dimension_semantics=("arbitrary", "arbitrary"),
            vmem_limit_bytes=VMEM_LIMIT_BYTES),
        name="swa",
    )


def _merge_kernel(x_ref, oraw_ref, gate_ref, gn_ref, oa_ref, gh_ref, ga_ref, wh_ref, wa_ref, wo_ref,
                  out_ref, mix_ref, oh_ref):
    gn = gn_ref[...]
    for rows in _row_blocks(x_ref.shape[0]):
        for h in range(HG_HEADS):
            sl = slice(h * HG_DIM, (h + 1) * HG_DIM)
            o = oraw_ref[rows, sl]
            hg = gate_ref[rows, sl].astype(F32)
            ms = jnp.mean(o * o, axis=-1, keepdims=True)
            y = o * lax.rsqrt(ms + EPS) * gn[:, sl]
            oh_ref[rows, sl] = (y * (hg * (1.0 + jnp.tanh(hg)))).astype(BF16)
    oh = oh_ref[...]
    oa = oa_ref[...]
    for ci in range(D_MODEL // MERGE_TN):
        cs = slice(ci * MERGE_TN, (ci + 1) * MERGE_TN)
        yh = _dot(oh, wh_ref[:, cs])
        ya = _dot(oa, wa_ref[:, cs])
        mix_ref[:, cs] = (_sigmoid(gh_ref[:, cs].astype(F32)) * yh
                          + _sigmoid(ga_ref[:, cs].astype(F32)) * ya).astype(BF16)
    out_ref[...] = x_ref[...] + _dot(mix_ref[...], wo_ref[...])


def _merge(x, o_raw, gn, oa, z, wh, wa, wo, next_w_in=None, next_layer=0):
    n = x.shape[0]
    tm = MERGE_TM
    resident = dict(pipeline_mode=pl.Buffered(1))
    rides = [] if next_w_in is None else [_ride_w_in(next_w_in, next_layer, n // tm, lambda i: i)]
    return _pallas_call_with_rides(
        _merge_kernel, rides, (x, o_raw, z, gn, oa, z, z, wh, wa, wo),
        out_shape=(jax.ShapeDtypeStruct((n, D_MODEL), F32),),
        grid=(n // tm,),
        in_specs=[
            pl.BlockSpec((tm, D_MODEL), lambda i: (i, 0)),
            pl.BlockSpec((tm, HG_WIDTH), lambda i: (i, 0)),
            pl.BlockSpec((tm, HG_WIDTH), lambda i: (i, _ZBLK_GATE)),
            pl.BlockSpec((1, HG_WIDTH), lambda i: (0, 0)),
            pl.BlockSpec((tm, AT_WIDTH), lambda i: (i, 0)),
            pl.BlockSpec((tm, D_MODEL), lambda i: (i, 0)),
            pl.BlockSpec((tm, D_MODEL), lambda i: (i, 1)),
            pl.BlockSpec((HG_WIDTH, D_MODEL), lambda i: (0, 0), **resident),
            pl.BlockSpec((AT_WIDTH, D_MODEL), lambda i: (0, 0), **resident),
            pl.BlockSpec((D_MODEL, D_MODEL), lambda i: (0, 0), **resident),
        ],
        out_specs=(pl.BlockSpec((tm, D_MODEL), lambda i: (i, 0)),),
        scratch_shapes=[pltpu.VMEM((tm, D_MODEL), BF16), pltpu.VMEM((tm, HG_WIDTH), BF16)],
        compiler_params=pltpu.CompilerParams(
            dimension_semantics=("parallel",),
            vmem_limit_bytes=VMEM_LIMIT_BYTES),
        name="merge",
    )


def _ffn_kernel(x_ref, g_ref, wu_ref, wd_ref, out_ref, h_ref):
    first = pl.program_id(1) == 0

    def accumulate(rows):
        for c in range(0, FFN_TF, FFN_SUB):
            u = jnp.maximum(_dot(h_ref[rows, :], wu_ref[:, c:c + FFN_SUB]), 0.0)
            out_ref[rows, :] += _dot((u * u).astype(BF16), wd_ref[c:c + FFN_SUB, :])

    @pl.when(first)
    def _():
        g = g_ref[...]
        for rows in _row_blocks(x_ref.shape[0]):
            x = x_ref[rows, :]
            h_ref[rows, :] = _rmsnorm_rows(x, g)
            out_ref[rows, :] = x
            accumulate(rows)

    @pl.when(jnp.logical_not(first))
    def _():
        accumulate(slice(None))


def _ffn(x, g, wu, wd):
    n = x.shape[0]
    tm, tf = FFN_TM, FFN_TF
    return pl.pallas_call(
        _ffn_kernel,
        out_shape=jax.ShapeDtypeStruct((n, D_MODEL), F32),
        grid=(n // tm, D_FF // tf),
        in_specs=[
            pl.BlockSpec((tm, D_MODEL), lambda i, f: (i, 0)),
            pl.BlockSpec((1, D_MODEL), lambda i, f: (0, 0)),
            pl.BlockSpec((D_MODEL, tf), lambda i, f: (0, f)),
            pl.BlockSpec((tf, D_MODEL), lambda i, f: (f, 0)),
        ],
        out_specs=pl.BlockSpec((tm, D_MODEL), lambda i, f: (i, 0)),
        scratch_shapes=[pltpu.VMEM((tm, D_MODEL), BF16)],
        compiler_params=pltpu.CompilerParams(
            dimension_semantics=("parallel", "arbitrary"),
            vmem_limit_bytes=VMEM_LIMIT_BYTES),
        name="ffn",
    )(x, g, wu, wd)


def kernel(x, norm_mix, w_in, lb_logits, hg_norm, q_norm, k_norm, sinks,
           w_hg_out, w_at_out, w_out, norm_ffn, w_up, w_down):
    bsz, t_len, _ = x.shape
    assert t_len % HGRN2_TB == 0 and t_len % SWA_TQ == 0
    assert (bsz * t_len) % INPROJ_TM == 0 and (bsz * t_len) % FFN_TM == 0
    xf = x.reshape(bsz * t_len, D_MODEL)
    wm, wkv = _convert_w_in(w_in, 0)
    for l in range(DEPTH):
        z, kv, w_hg, w_at, w_o = _inproj(xf, norm_mix[l].reshape(1, D_MODEL), wm, wkv,
                                         (w_hg_out, w_at_out, w_out), l)
        o_hg, wu = _hgrn2(z, lb_logits, l, bsz, t_len, (w_up,))
        o_at, wd = _swa(z, kv, jnp.tile(q_norm[l], 2).reshape(1, LANES),
                        jnp.tile(k_norm[l], 2).reshape(1, LANES), sinks[l], bsz, t_len,
                        (w_down,), l)
        gn = jnp.tile(hg_norm[l], HG_HEADS).reshape(1, HG_WIDTH)
        if l + 1 < DEPTH:
            x1, wm, wkv = _merge(xf, o_hg, gn, o_at, z, w_hg, w_at, w_o, w_in, l + 1)
        else:
            x1, = _merge(xf, o_hg, gn, o_at, z, w_hg, w_at, w_o)
        xf = _ffn(x1, norm_ffn[l].reshape(1, D_MODEL), wu, wd)
    return xf.reshape(bsz, t_len, D_MODEL)
```

```python
import functools
from typing import Callable, NamedTuple

import jax
import jax.numpy as jnp
import numpy as np
from jax import lax
from jax.experimental import pallas as pl
from jax.experimental.pallas import tpu as pltpu

F32 = jnp.float32
BF16 = jnp.bfloat16

D_MODEL = 2048
DEPTH = 2
HG_HEADS = 8
HG_DIM = 128
HG_WIDTH = HG_HEADS * HG_DIM
HG_CHUNK = 64
AT_HEADS = 16
AT_KV_HEADS = 2
AT_GROUP = AT_HEADS // AT_KV_HEADS
AT_HEAD_DIM = 64
AT_WIDTH = AT_HEADS * AT_HEAD_DIM
AT_KV_WIDTH = AT_KV_HEADS * AT_HEAD_DIM
WINDOW = 128
D_FF = 4 * D_MODEL
EPS = 1e-6
LOG2E = 1.4426950408889634

_OFF_HG = 0
_OFF_KV = 4 * HG_WIDTH + AT_WIDTH
_OFF_GATES = _OFF_KV + 2 * AT_KV_WIDTH
Z_WIDTH = 2 * D_MODEL + 4 * HG_WIDTH + AT_WIDTH
Z_DTYPE = BF16
_ZBLK_Q, _ZBLK_F, _ZBLK_I, _ZBLK_GATE, _ZBLK_AQ = 4, 5, 6, 7, 8
_HALVED_BLOCKS = (0, 1, 3)

LANES = 128
VMEM_LIMIT_BYTES = 60 * 1024 * 1024

INPROJ_TM, INPROJ_TN = 1024, 2304
HGRN2_TB = 1024
HGRN2_HEAD_GROUP = 8
HGRN2_LANE_BLOCK = 256
SWA_TQ = 512
MERGE_TM, MERGE_TN = 512, 512
FFN_TM, FFN_TF, FFN_SUB = 1024, 1024, 512
NORM_ROWS = 256


def _dot(a, b):
    return jnp.dot(a, b, preferred_element_type=F32)


def _dot_nt(a, b):
    return lax.dot_general(a, b, (((1,), (1,)), ((), ())), preferred_element_type=F32)


def _dot_tn(a, b):
    return lax.dot_general(a, b, (((0,), (0,)), ((), ())), preferred_element_type=F32)


def _sigmoid(x):
    return 0.5 + 0.5 * jnp.tanh(0.5 * x)


def _silu(x):
    h = 0.5 * x
    return h * (1.0 + jnp.tanh(h))


def _split_bf16(x):
    hi = x.astype(BF16)
    lo = (x - hi.astype(F32)).astype(BF16)
    return hi, lo


def _rmsnorm_rows(x, g):
    ms = jnp.mean(x * x, axis=-1, keepdims=True)
    return (x * lax.rsqrt(ms + EPS) * g).astype(BF16)


def _row_blocks(n):
    return [slice(r, r + NORM_ROWS) for r in range(0, n, NORM_ROWS)]


class _Ride(NamedTuple):
    arrays: tuple
    in_specs: tuple
    out_shapes: tuple
    out_specs: tuple
    body: Callable


def _cast_body(w_ref, o_ref):
    o_ref[...] = w_ref[...].astype(o_ref.dtype)


def _w_in_body(w_ref, wm_ref, wkv_ref):
    wm_ref[:, :2 * D_MODEL] = (w_ref[:, _OFF_GATES:] * 0.5).astype(BF16)
    for k in range(5):
        src = slice(k * HG_WIDTH, (k + 1) * HG_WIDTH)
        dst = slice(2 * D_MODEL + k * HG_WIDTH, 2 * D_MODEL + (k + 1) * HG_WIDTH)
        scale = 0.5 if k in _HALVED_BLOCKS else 1.0
        wm_ref[:, dst] = (w_ref[:, src] * scale).astype(BF16)
    wkv_ref[...] = w_ref[:, _OFF_KV:_OFF_GATES].astype(BF16)


def _ride_cast(w, layer, nblk, block_of):
    _, rows, cols = w.shape
    rb = rows // nblk
    assert rb * nblk == rows and rb % 16 == 0
    return _Ride(
        (w,),
        (pl.BlockSpec((None, rb, cols), lambda *g: (layer, block_of(*g), 0)),),
        (jax.ShapeDtypeStruct((rows, cols), BF16),),
        (pl.BlockSpec((rb, cols), lambda *g: (block_of(*g), 0)),),
        _cast_body)


def _ride_w_in(w_in, layer, nblk, block_of):
    _, rows, cols = w_in.shape
    rb = rows // nblk
    assert rb * nblk == rows and rb % 16 == 0
    return _Ride(
        (w_in,),
        (pl.BlockSpec((None, rb, cols), lambda *g: (layer, block_of(*g), 0)),),
        (jax.ShapeDtypeStruct((rows, Z_WIDTH), BF16),
         jax.ShapeDtypeStruct((rows, 2 * AT_KV_WIDTH), BF16)),
        (pl.BlockSpec((rb, Z_WIDTH), lambda *g: (block_of(*g), 0)),
         pl.BlockSpec((rb, 2 * AT_KV_WIDTH), lambda *g: (block_of(*g), 0))),
        _w_in_body)


def _pallas_call_with_rides(body, rides, args, *, in_specs, out_shape, out_specs,
                            once_per_row_tile=False, **kwargs):
    n_in, n_out = len(in_specs), len(out_shape)
    n_rin = [len(r.arrays) for r in rides]
    n_rout = [len(r.out_shapes) for r in rides]

    def kernel(*refs):
        pos = n_in
        r_ins = []
        for k in n_rin:
            r_ins.append(refs[pos:pos + k])
            pos += k
        outs = refs[pos:pos + n_out]
        pos += n_out
        r_outs = []
        for k in n_rout:
            r_outs.append(refs[pos:pos + k])
            pos += k
        body(*refs[:n_in], *outs, *refs[pos:])

        def run_rides():
            for r, i_refs, o_refs in zip(rides, r_ins, r_outs):
                r.body(*i_refs, *o_refs)

        if rides and once_per_row_tile:
            pl.when(pl.program_id(1) == 0)(run_rides)
        else:
            run_rides()

    return pl.pallas_call(
        kernel,
        in_specs=list(in_specs) + [s for r in rides for s in r.in_specs],
        out_shape=tuple(out_shape) + tuple(s for r in rides for s in r.out_shapes),
        out_specs=tuple(out_specs) + tuple(s for r in rides for s in r.out_specs),
        **kwargs,
    )(*args, *[a for r in rides for a in r.arrays])


def _convert_w_in(w_in, layer):
    nblk = 16
    ride = _ride_w_in(w_in, layer, nblk, lambda i: i)
    return pl.pallas_call(
        _w_in_body,
        grid=(nblk,),
        in_specs=list(ride.in_specs),
        out_shape=ride.out_shapes,
        out_specs=ride.out_specs,
        compiler_params=pltpu.CompilerParams(
            dimension_semantics=("parallel",),
            vmem_limit_bytes=VMEM_LIMIT_BYTES),
        name="convert_w_in",
    )(w_in)


def _inproj_kernel(x_ref, g_ref, wm_ref, wkv_ref, z_ref, kv_ref, h_ref):
    first = pl.program_id(1) == 0

    @pl.when(first)
    def _():
        g = g_ref[...]
        blocks = _row_blocks(x_ref.shape[0])

        def norm(rows):
            h_ref[rows, :] = _rmsnorm_rows(x_ref[rows, :], g)

        norm(blocks[0])
        for b, rows in enumerate(blocks):
            if b + 1 < len(blocks):
                norm(blocks[b + 1])
            kv_ref[rows, :] = _dot(h_ref[rows, :], wkv_ref[...])
            z_ref[rows, :] = _dot(h_ref[rows, :], wm_ref[...]).astype(z_ref.dtype)

    @pl.when(jnp.logical_not(first))
    def _():
        z_ref[...] = _dot(h_ref[...], wm_ref[...]).astype(z_ref.dtype)


def _inproj(x, g, wm, wkv, ride_weights, layer):
    n = x.shape[0]
    tm, tn = INPROJ_TM, INPROJ_TN
    rides = [_ride_cast(w, layer, n // tm, lambda i, j: i) for w in ride_weights]
    return _pallas_call_with_rides(
        _inproj_kernel, rides, (x, g, wm, wkv),
        once_per_row_tile=True,
        out_shape=(jax.ShapeDtypeStruct((n, Z_WIDTH), Z_DTYPE),
                   jax.ShapeDtypeStruct((n, 2 * AT_KV_WIDTH), F32)),
        grid=(n // tm, Z_WIDTH // tn),
        in_specs=[
            pl.BlockSpec((tm, D_MODEL), lambda i, j: (i, 0)),
            pl.BlockSpec((1, D_MODEL), lambda i, j: (0, 0)),
            pl.BlockSpec((D_MODEL, tn), lambda i, j: (0, j)),
            pl.BlockSpec((D_MODEL, 2 * AT_KV_WIDTH), lambda i, j: (0, 0)),
        ],
        out_specs=(
            pl.BlockSpec((tm, tn), lambda i, j: (i, j)),
            pl.BlockSpec((tm, 2 * AT_KV_WIDTH), lambda i, j: (i, 0)),
        ),
        scratch_shapes=[pltpu.VMEM((tm, D_MODEL), BF16)],
        compiler_params=pltpu.CompilerParams(
            dimension_semantics=("parallel", "arbitrary"),
            vmem_limit_bytes=VMEM_LIMIT_BYTES),
        name="inproj",
    )


def _hgrn2_kernel(lbl_ref, q_ref, f_ref, i_ref, o_ref, st_ref, ops_ref, dl_ref, *, layer):
    @pl.when(pl.program_id(1) == 0)
    def _():
        st_ref[...] = jnp.zeros_like(st_ref)

    lbl = lbl_ref[...]
    e = jnp.exp(lbl - jnp.max(lbl, axis=0, keepdims=True))
    p = e / jnp.sum(e, axis=0, keepdims=True)
    lb = jnp.zeros((1, HG_WIDTH), F32)
    for l in range(1, layer + 1):
        lb = lb + p[l:l + 1, :]
    fb = 0.5 * (1.0 - lb)
    fa = lb + fb
    fc = 1.0 - fa

    c = HG_CHUNK
    row = lax.broadcasted_iota(jnp.int32, (c, c), 0)
    col = lax.broadcasted_iota(jnp.int32, (c, c), 1)
    causal = row >= col
    tril = causal.astype(BF16)

    def operands(rows, slot):
        for c0 in range(0, HG_WIDTH, HGRN2_LANE_BLOCK):
            cols = slice(c0, c0 + HGRN2_LANE_BLOCK)
            hq = q_ref[rows, cols].astype(F32)
            qf = hq * (1.0 + jnp.tanh(hq))
            bt = fb[:, cols] * jnp.tanh(f_ref[rows, cols].astype(F32))
            fg = fa[:, cols] + bt
            kf = fc[:, cols] - bt
            hi, lo = _split_bf16(jnp.log(fg) * LOG2E)
            cum = _dot(tril, hi) + _dot(tril, lo)
            ref = cum[c // 2 - 1:c // 2, :]
            last = cum[c - 1:c, :]
            e1 = jnp.exp2(cum - ref)
            qa_f = qf * e1
            ka_f = kf / e1
            ops_ref[slot, 0, :, cols] = qa_f.astype(BF16)
            ops_ref[slot, 1, :, cols] = ka_f.astype(BF16)
            ops_ref[slot, 2, :, cols] = (qa_f * jnp.exp2(ref)).astype(BF16)
            ops_ref[slot, 3, :, cols] = (ka_f * jnp.exp2(last - ref)).astype(BF16)
            dl_ref[slot, :, cols] = jnp.exp2(last)

    heads = [(h, slice(h * HG_DIM, (h + 1) * HG_DIM)) for h in range(HG_HEADS)]

    def recurrence(rows, slot):
        dl = dl_ref[slot]
        a = [_dot_nt(ops_ref[slot, 0, :, sl], ops_ref[slot, 1, :, sl]) for _, sl in heads]
        st = [st_ref[h] for h, _ in heads]
        o_inter = [_dot_nt(ops_ref[slot, 2, :, sl], s.astype(BF16)) for (_, sl), s in zip(heads, st)]
        vb = [i_ref[rows, sl].astype(BF16) for _, sl in heads]
        for (h, sl), s, v in zip(heads, st, vb):
            st_ref[h] = _dot_tn(v, ops_ref[slot, 3, :, sl]) + s * dl[:, sl]
        a = [jnp.where(causal, x, 0.0).astype(BF16) for x in a]
        for (_, sl), x, v, oi in zip(heads, a, vb, o_inter):
            o_ref[rows, sl] = _dot(x, v) + oi

    n_chunks = q_ref.shape[0] // c
    operands(slice(0, c), 0)
    for k in range(n_chunks):
        if k + 1 < n_chunks:
            operands(slice((k + 1) * c, (k + 2) * c), (k + 1) % 2)
        recurrence(slice(k * c, (k + 1) * c), k % 2)


def _hgrn2(z, lb_logits, layer, bsz, t_len, ride_weights=()):
    n = z.shape[0]
    tb = HGRN2_TB
    nt = t_len // tb
    rides = [_ride_cast(w, layer, bsz * nt, lambda b, t: b * nt + t) for w in ride_weights]

    def zspec(blk):
        return pl.BlockSpec((tb, HG_WIDTH), lambda b, t: (b * nt + t, blk))

    return _pallas_call_with_rides(
        functools.partial(_hgrn2_kernel, layer=layer), rides, (lb_logits, z, z, z),
        out_shape=(jax.ShapeDtypeStruct((n, HG_WIDTH), F32),),
        grid=(bsz, nt),
        in_specs=[
            pl.BlockSpec((DEPTH, HG_WIDTH), lambda b, t: (0, 0)),
            zspec(_ZBLK_Q), zspec(_ZBLK_F), zspec(_ZBLK_I),
        ],
        out_specs=(pl.BlockSpec((tb, HG_WIDTH), lambda b, t: (b * nt + t, 0)),),
        scratch_shapes=[pltpu.VMEM((HG_HEADS, HG_DIM, HG_DIM), F32),
                        pltpu.VMEM((2, 4, HG_CHUNK, HG_WIDTH), BF16),
                        pltpu.VMEM((2, 1, HG_WIDTH), F32)],
        compiler_params=pltpu.CompilerParams(
            dimension_semantics=("parallel", "arbitrary"),
            vmem_limit_bytes=VMEM_LIMIT_BYTES),
        name="hgrn2",
    )


def _alibi_slope(h):
    return float(np.float32(2.0) ** np.float32(-8.0 * (h + 1.0) / AT_HEADS))


def _swa_kernel(sink_ref, qn_ref, kn_ref, q_ref, kvc_ref, kvp_ref, o_ref, bias_ref):
    w = WINDOW
    tq = q_ref.shape[0]
    first_of_seq = pl.program_id(1) == 0

    @pl.when((pl.program_id(0) == 0) & first_of_seq)
    def _():
        qi = lax.broadcasted_iota(jnp.int32, (w, 2 * w), 0)
        kj = lax.broadcasted_iota(jnp.int32, (w, 2 * w), 1)
        dist = qi + w - kj
        in_band = (dist >= 0) & (dist < w)
        in_band_cur = in_band & (kj >= w)
        distf = dist.astype(F32)
        for h in range(AT_HEADS):
            b = (-_alibi_slope(h) * LOG2E) * distf
            sink = sink_ref[h] * LOG2E
            bias_ref[0, h] = jnp.where(kj == 0, sink, jnp.where(in_band, b, -jnp.inf))
            bias_ref[1, h] = jnp.where(kj == 0, sink, jnp.where(in_band_cur, b, -jnp.inf))

    lane = lax.broadcasted_iota(jnp.int32, (1, LANES), 1)
    lo_half = lane < AT_HEAD_DIM
    r = lax.broadcasted_iota(jnp.int32, (2 * LANES, 2 * LANES), 0)
    cidx = lax.broadcasted_iota(jnp.int32, (2 * LANES, 2 * LANES), 1)
    same_head = lax.shift_right_logical(r, 6) == lax.shift_right_logical(cidx, 6)
    seg_mean = jnp.where(same_head, 1.0 / AT_HEAD_DIM, 0.0).astype(BF16)

    kvc = kvc_ref[...]
    kvp = kvp_ref[...]
    k_all = jnp.concatenate([kvp[:, :LANES], kvc[:, :LANES]], axis=0)
    v_all = jnp.concatenate([kvp[:, LANES:], kvc[:, LANES:]], axis=0)
    hi, lo = _split_bf16(k_all * k_all)
    k_msq = _dot(hi, seg_mean[:LANES, :LANES]) + _dot(lo, seg_mean[:LANES, :LANES])
    kg = kn_ref[...] * qn_ref[...] * (AT_HEAD_DIM ** -0.5 * LOG2E)
    kn = k_all * lax.rsqrt(k_msq + EPS) * kg
    k_sw = pltpu.roll(kn, AT_HEAD_DIM, 1)
    v_sw = pltpu.roll(v_all, AT_HEAD_DIM, 1)
    keys = ((jnp.where(lo_half, kn, 0.0).astype(BF16), jnp.where(lo_half, 0.0, k_sw).astype(BF16)),
            (jnp.where(lo_half, k_sw, 0.0).astype(BF16), jnp.where(lo_half, 0.0, kn).astype(BF16)))
    vals = ((jnp.where(lo_half, v_all, 1.0).astype(BF16), jnp.where(lo_half, 1.0, v_sw).astype(BF16)),
            (jnp.where(lo_half, v_sw, 1.0).astype(BF16), jnp.where(lo_half, 1.0, v_all).astype(BF16)))

    row0 = lax.broadcasted_iota(jnp.int32, (16, LANES), 0) == 0
    lo16 = lax.broadcasted_iota(jnp.int32, (16, LANES), 1) < AT_HEAD_DIM
    zero16 = jnp.zeros((16, LANES), BF16)

    def with_row0_zeroed(x, where_zero):
        return jnp.concatenate([jnp.where(where_zero, zero16, x[:16]), x[16:]], axis=0)

    half = AT_GROUP // 2
    units = [(qb, kvh, par) for qb in range(tq // w) for kvh in range(AT_KV_HEADS) for par in range(2)]
    scores = {}
    for qb in range(tq // w):
        rows = slice(qb * w, (qb + 1) * w)
        krows = slice(qb * w, (qb + 2) * w)
        qh = []
        for pair in range(AT_WIDTH // (2 * LANES)):
            x = q_ref[rows, pair * 2 * LANES:(pair + 1) * 2 * LANES].astype(F32)
            msq = _dot((x * x).astype(BF16), seg_mean)
            xn = (x * lax.rsqrt(msq + EPS)).astype(BF16)
            qh += [xn[:, :LANES], xn[:, LANES:]]
        for kvh in range(AT_KV_HEADS):
            lhs = jnp.concatenate(qh[kvh * half:(kvh + 1) * half], axis=0)
            for par in range(2):
                k_blk = with_row0_zeroed(keys[kvh][par][krows], row0)
                scores[qb, kvh, par] = _dot_nt(lhs, k_blk)
    res = {}
    for qb, kvh, par in units:
        krows = slice(qb * w, (qb + 2) * w)
        sel = jnp.where(first_of_seq, 1, 0) if qb == 0 else 0
        v_blk = with_row0_zeroed(vals[kvh][par][krows], row0 & (lo16 if par == 0 else ~lo16))
        pb = []
        for j in range(half):
            head = kvh * AT_GROUP + 2 * j + par
            s = scores[qb, kvh, par][j * w:(j + 1) * w] + bias_ref[sel, head]
            pb.append(jnp.exp2(s - jnp.max(s, axis=-1, keepdims=True)).astype(BF16))
        res[qb, kvh, par] = _dot(jnp.concatenate(pb, axis=0), v_blk)
    for qb in range(tq // w):
        rows = slice(qb * w, (qb + 1) * w)
        for kvh in range(AT_KV_HEADS):
            for j in range(half):
                tile = kvh * half + j
                re = res[qb, kvh, 0][j * w:(j + 1) * w]
                ro = res[qb, kvh, 1][j * w:(j + 1) * w]
                num = jnp.where(lo_half, re, ro)
                den = pltpu.roll(jnp.where(lo_half, ro, re), AT_HEAD_DIM, 1)
                o_ref[rows, tile * LANES:(tile + 1) * LANES] = (num / den).astype(o_ref.dtype)


def _swa(z, kv, qn, kn, sinks, bsz, t_len, ride_weights=(), layer=0):
    n = z.shape[0]
    tq = SWA_TQ
    nt = t_len // tq
    per = tq // WINDOW
    rides = [_ride_cast(w, layer, bsz * nt, lambda b, t: b * nt + t) for w in ride_weights]
    return _pallas_call_with_rides(
        _swa_kernel, rides, (sinks, qn, kn, z, kv, kv),
        out_shape=(jax.ShapeDtypeStruct((n, AT_WIDTH), BF16),),
        grid=(bsz, nt),
        in_specs=[
            pl.BlockSpec(memory_space=pltpu.SMEM),
            pl.BlockSpec((1, LANES), lambda b, t: (0, 0)),
            pl.BlockSpec((1, LANES), lambda b, t: (0, 0)),
            pl.BlockSpec((tq, AT_WIDTH), lambda b, t: (b * nt + t, _ZBLK_AQ)),
            pl.BlockSpec((tq, 2 * AT_KV_WIDTH), lambda b, t: (b * nt + t, 0)),
            pl.BlockSpec((WINDOW, 2 * AT_KV_WIDTH),
                         lambda b, t: (jnp.maximum((b * nt + t) * per - 1, 0), 0)),
        ],
        out_specs=(pl.BlockSpec((tq, AT_WIDTH), lambda b, t: (b * nt + t, 0)),),
        scratch_shapes=[pltpu.VMEM((2, AT_HEADS, WINDOW, 2 * WINDOW), F32)],
        compiler_params=pltpu.CompilerParams(
            dimension_semantics=("arbitrary", "arbitrary"),
            vmem_limit_bytes=VMEM_LIMIT_BYTES),
        name="swa",
    )


def _merge_kernel(x_ref, oraw_ref, gate_ref, gn_ref, oa_ref, g_ref, wh_ref, wa_ref, wo_ref,
                  out_ref, mix_ref, oh_ref):
    gn = gn_ref[...]
    blocks = _row_blocks(x_ref.shape[0])

    def head_norm(rows):
        for h in range(HG_HEADS):
            sl = slice(h * HG_DIM, (h + 1) * HG_DIM)
            o = oraw_ref[rows, sl]
            hg = gate_ref[rows, sl].astype(F32)
            ms = jnp.mean(o * o, axis=-1, keepdims=True)
            y = o * lax.rsqrt(ms + EPS) * gn[:, sl]
            oh_ref[rows, sl] = (y * (hg * (1.0 + jnp.tanh(hg)))).astype(BF16)

    head_norm(blocks[0])
    for b, rows in enumerate(blocks):
        if b + 1 < len(blocks):
            head_norm(blocks[b + 1])
        oh = oh_ref[rows, :]
        oa = oa_ref[rows, :]
        for ci in range(D_MODEL // MERGE_TN):
            cs = slice(ci * MERGE_TN, (ci + 1) * MERGE_TN)
            yh = _dot(oh, wh_ref[:, cs])
            ya = _dot(oa, wa_ref[:, cs])
            cs_a = slice(D_MODEL + ci * MERGE_TN, D_MODEL + (ci + 1) * MERGE_TN)
            sig_h = 0.5 + 0.5 * jnp.tanh(g_ref[rows, cs].astype(F32))
            sig_a = 0.5 + 0.5 * jnp.tanh(g_ref[rows, cs_a].astype(F32))
            mix_ref[rows, cs] = (sig_h * yh + sig_a * ya).astype(BF16)
        out_ref[rows, :] = x_ref[rows, :] + _dot(mix_ref[rows, :], wo_ref[...])


def _merge(x, o_raw, gn, oa, z, wh, wa, wo, next_w_in=None, next_layer=0):
    n = x.shape[0]
    tm = MERGE_TM
    resident = dict(pipeline_mode=pl.Buffered(1))
    rides = [] if next_w_in is None else [_ride_w_in(next_w_in, next_layer, n // tm, lambda i: i)]
    return _pallas_call_with_rides(
        _merge_kernel, rides, (x, o_raw, z, gn, oa, z, wh, wa, wo),
        out_shape=(jax.ShapeDtypeStruct((n, D_MODEL), F32),),
        grid=(n // tm,),
        in_specs=[
            pl.BlockSpec((tm, D_MODEL), lambda i: (i, 0)),
            pl.BlockSpec((tm, HG_WIDTH), lambda i: (i, 0)),
            pl.BlockSpec((tm, HG_WIDTH), lambda i: (i, _ZBLK_GATE)),
            pl.BlockSpec((1, HG_WIDTH), lambda i: (0, 0)),
            pl.BlockSpec((tm, AT_WIDTH), lambda i: (i, 0)),
            pl.BlockSpec((tm, 2 * D_MODEL), lambda i: (i, 0)),
            pl.BlockSpec((HG_WIDTH, D_MODEL), lambda i: (0, 0), **resident),
            pl.BlockSpec((AT_WIDTH, D_MODEL), lambda i: (0, 0), **resident),
            pl.BlockSpec((D_MODEL, D_MODEL), lambda i: (0, 0), **resident),
        ],
        out_specs=(pl.BlockSpec((tm, D_MODEL), lambda i: (i, 0)),),
        scratch_shapes=[pltpu.VMEM((tm, D_MODEL), BF16), pltpu.VMEM((tm, HG_WIDTH), BF16)],
        compiler_params=pltpu.CompilerParams(
            dimension_semantics=("parallel",),
            vmem_limit_bytes=VMEM_LIMIT_BYTES),
        name="merge",
    )


def _ffn_kernel(x_ref, g_ref, wu_ref, wd_ref, out_ref, h_ref):
    first = pl.program_id(1) == 0

    def accumulate(rows):
        for c in range(0, FFN_TF, FFN_SUB):
            u = jnp.maximum(_dot(h_ref[rows, :], wu_ref[:, c:c + FFN_SUB]), 0.0)
            out_ref[rows, :] += _dot((u * u).astype(BF16), wd_ref[c:c + FFN_SUB, :])

    @pl.when(first)
    def _():
        g = g_ref[...]
        blocks = _row_blocks(x_ref.shape[0])

        def prepare(rows):
            x = x_ref[rows, :]
            h_ref[rows, :] = _rmsnorm_rows(x, g)
            out_ref[rows, :] = x

        prepare(blocks[0])
        for b, rows in enumerate(blocks):
            if b + 1 < len(blocks):
                prepare(blocks[b + 1])
            accumulate(rows)

    @pl.when(jnp.logical_not(first))
    def _():
        accumulate(slice(None))


def _ffn(x, g, wu, wd):
    n = x.shape[0]
    tm, tf = FFN_TM, FFN_TF
    return pl.pallas_call(
        _ffn_kernel,
        out_shape=jax.ShapeDtypeStruct((n, D_MODEL), F32),
        grid=(n // tm, D_FF // tf),
        in_specs=[
            pl.BlockSpec((tm, D_MODEL), lambda i, f: (i, 0)),
            pl.BlockSpec((1, D_MODEL), lambda i, f: (0, 0)),
            pl.BlockSpec((D_MODEL, tf), lambda i, f: (0, f)),
            pl.BlockSpec((tf, D_MODEL), lambda i, f: (f, 0)),
        ],
        out_specs=pl.BlockSpec((tm, D_MODEL), lambda i, f: (i, 0)),
        scratch_shapes=[pltpu.VMEM((tm, D_MODEL), BF16)],
        compiler_params=pltpu.CompilerParams(
            dimension_semantics=("parallel", "arbitrary"),
            vmem_limit_bytes=VMEM_LIMIT_BYTES),
        name="ffn",
    )(x, g, wu, wd)


def kernel(x, norm_mix, w_in, lb_logits, hg_norm, q_norm, k_norm, sinks,
           w_hg_out, w_at_out, w_out, norm_ffn, w_up, w_down):
    bsz, t_len, _ = x.shape
    assert t_len % HGRN2_TB == 0 and t_len % SWA_TQ == 0
    assert (bsz * t_len) % INPROJ_TM == 0 and (bsz * t_len) % FFN_TM == 0
    xf = x.reshape(bsz * t_len, D_MODEL)
    wm, wkv = _convert_w_in(w_in, 0)
    for l in range(DEPTH):
        z, kv, w_hg, w_at, w_o = _inproj(xf, norm_mix[l].reshape(1, D_MODEL), wm, wkv,
                                         (w_hg_out, w_at_out, w_out), l)
        o_hg, wu = _hgrn2(z, lb_logits, l, bsz, t_len, (w_up,))
        o_at, wd = _swa(z, kv, jnp.tile(q_norm[l], 2).reshape(1, LANES),
                        jnp.tile(k_norm[l], 2).reshape(1, LANES), sinks[l], bsz, t_len,
                        (w_down,), l)
        gn = jnp.tile(hg_norm[l], HG_HEADS).reshape(1, HG_WIDTH)
        if l + 1 < DEPTH:
            x1, wm, wkv = _merge(xf, o_hg, gn, o_at, z, w_hg, w_at, w_o, w_in, l + 1)
        else:
            x1, = _merge(xf, o_hg, gn, o_at, z, w_hg, w_at, w_o)
        xf = _ffn(x1, norm_ffn[l].reshape(1, D_MODEL), wu, wd)
    return xf.reshape(bsz, t_len, D_MODEL)
```
